```python
import jax, jax.numpy as jnp
from jax import lax
import numpy as np

D_MODEL = 1024
BATCH = 2
SEQ = 8192
DEPTH = 1

D_MIX = D_MODEL
D_CONV = D_MIX // 2
D_ATTN = D_MIX - D_CONV
HEAD_DIM = 64
N_HEADS = D_ATTN // HEAD_DIM
CONV_KERNEL = 31
DILATED_BRANCHES = ((128, 1), (512, 4), (2048, 16))
ROPE_THETA = 10000.0
N_EXPERTS = 32
TOP_K = 4
D_FF = D_MODEL
SWIGLU_LIMIT = 7.0
SWIGLU_ALPHA = 1.702
EXPERT_BLOCK = 128
LN_EPS = 1e-5
DEEPNORM_ALPHA = float((2 * DEPTH) ** 0.25)
DEEPNORM_BETA = float((8 * DEPTH) ** -0.25)
D_IN_PROJ = 2 * D_CONV + 3 * D_ATTN

kernel_name = "hymba_conformer_dilated_moe_deepnorm"


def layer_norm(x, g, b):
    xf = x.astype(jnp.float32)
    mu = jnp.mean(xf, axis=-1, keepdims=True)
    var = jnp.mean(jnp.square(xf - mu), axis=-1, keepdims=True)
    y = (xf - mu) * lax.rsqrt(var + LN_EPS) * g.astype(jnp.float32) + b.astype(jnp.float32)
    return y.astype(x.dtype)


def rope(t, positions):
    e = t.shape[-1]
    inv_freq = ROPE_THETA ** (-jnp.arange(0, e, 2, dtype=jnp.float32) / e)
    ang = positions.astype(jnp.float32)[..., None] * inv_freq
    cos = jnp.concatenate([jnp.cos(ang), jnp.cos(ang)], axis=-1)[:, :, None, :]
    sin = jnp.concatenate([jnp.sin(ang), jnp.sin(ang)], axis=-1)[:, :, None, :]
    t1, t2 = jnp.split(t.astype(jnp.float32), 2, axis=-1)
    rot = jnp.concatenate([-t2, t1], axis=-1)
    return (t.astype(jnp.float32) * cos + rot * sin).astype(t.dtype)


def dilated_branch(q, k, v, window, dilation):
    b, s, h, e = q.shape
    w = window // dilation
    sub_len = s // dilation
    n_blk = -(-sub_len // w)
    pad_len = n_blk * w

    def to_sub(t):
        t = t.reshape(b, sub_len, dilation, h, e).transpose(0, 2, 3, 1, 4)
        t = jnp.pad(t, ((0, 0), (0, 0), (0, 0), (0, pad_len - sub_len), (0, 0)))
        return t.reshape(b, dilation, h, n_blk, w, e)

    def with_prev(t):
        prev = jnp.pad(t, ((0, 0), (0, 0), (0, 0), (1, 0), (0, 0), (0, 0)))[:, :, :, :-1]
        return jnp.concatenate([prev, t], axis=4)

    qs = to_sub(q)
    kb = with_prev(to_sub(k))
    vb = with_prev(to_sub(v))
    scale = 1.0 / np.sqrt(e)
    scores = jnp.einsum('brhnqe,brhnke->brhnqk', qs.astype(jnp.float32),
                        kb.astype(jnp.float32)) * scale
    qi = jnp.arange(w)[:, None]
    ki = jnp.arange(2 * w)[None, :]
    rel = qi + w - ki
    band = (rel >= 0) & (rel <= w)
    blk = jnp.arange(n_blk)[:, None, None]
    mask = band[None] & ((blk > 0) | (ki[None] >= w))
    scores = jnp.where(mask, scores, -jnp.inf)
    m = jnp.max(scores, axis=-1, keepdims=True)
    p = jnp.exp(scores - m)
    den = jnp.sum(p, axis=-1)
    o = jnp.einsum('brhnqk,brhnke->brhnqe', p, vb.astype(jnp.float32)) / den[..., None]
    lse = m[..., 0] + jnp.log(den)
    o = o.reshape(b, dilation, h, pad_len, e)[:, :, :, :sub_len]
    o = o.transpose(0, 3, 1, 2, 4).reshape(b, s, h, e)
    lse = lse.reshape(b, dilation, h, pad_len)[..., :sub_len]
    lse = lse.transpose(0, 3, 1, 2).reshape(b, s, h)
    return o, lse


def hybrid_mixer(x, positions, w_in, conv_w, conv_b, conv_ln_g, conv_ln_b,
                 conv_pw_w, conv_pw_b, w_out):
    b, s, _ = x.shape
    proj = x @ w_in
    splits = np.cumsum([D_CONV, D_CONV, D_ATTN, D_ATTN])
    a, gt, q, k, v = jnp.split(proj, splits, axis=-1)

    u = a * jax.nn.sigmoid(gt)
    u = lax.conv_general_dilated(
        u, conv_w[:, None, :], window_strides=(1,),
        padding=[(CONV_KERNEL - 1, 0)],
        dimension_numbers=('NWC', 'WIO', 'NWC'),
        feature_group_count=D_CONV) + conv_b
    u = jax.nn.silu(layer_norm(u, conv_ln_g, conv_ln_b))
    conv_out = u @ conv_pw_w + conv_pw_b

    q = rope(q.reshape(b, s, N_HEADS, HEAD_DIM), positions)
    k = rope(k.reshape(b, s, N_HEADS, HEAD_DIM), positions)
    v = v.reshape(b, s, N_HEADS, HEAD_DIM)
    outs = []
    lses = []
    for window, dilation in DILATED_BRANCHES:
        o, l = dilated_branch(q, k, v, window, dilation)
        outs.append(o)
        lses.append(l)
    wts = jax.nn.softmax(jnp.stack(lses, axis=0), axis=0)
    attn = jnp.sum(jnp.stack(outs, axis=0) * wts[..., None], axis=0)
    attn_out = attn.reshape(b, s, D_ATTN).astype(x.dtype)

    return jnp.concatenate([conv_out, attn_out], axis=-1) @ w_out


def moe_ffn(x, router_w, router_b, w_gate, b_gate, w_up, b_up, w_down, b_down):
    b, s, dm = x.shape
    xt = x.reshape(-1, dm)
    n_tok = xt.shape[0]
    logits = (xt @ router_w + router_b).astype(jnp.float32)
    top_logit, top_idx = lax.top_k(logits, TOP_K)
    gate = jax.nn.softmax(top_logit, axis=-1)

    n_assign = n_tok * TOP_K
    flat_e = top_idx.reshape(-1)
    flat_tok = jnp.repeat(jnp.arange(n_tok, dtype=jnp.int32), TOP_K)
    flat_g = gate.reshape(-1)
    order = jnp.argsort(flat_e, stable=True)
    sorted_e = flat_e[order]
    counts = jnp.bincount(flat_e, length=N_EXPERTS)
    padded = (counts + EXPERT_BLOCK - 1) // EXPERT_BLOCK * EXPERT_BLOCK
    starts = jnp.cumsum(counts) - counts
    pends = jnp.cumsum(padded)
    pstarts = pends - padded
    dest = pstarts[sorted_e] + (jnp.arange(n_assign) - starts[sorted_e])
    n_rows = (-(-n_assign // EXPERT_BLOCK)) * EXPERT_BLOCK + N_EXPERTS * EXPERT_BLOCK
    n_blocks = n_rows // EXPERT_BLOCK
    buf_tok = jnp.zeros((n_rows,), jnp.int32).at[dest].set(flat_tok[order])
    buf_g = jnp.zeros((n_rows,), jnp.float32).at[dest].set(flat_g[order])
    blk_expert = jnp.minimum(
        jnp.searchsorted(pends, jnp.arange(n_blocks) * EXPERT_BLOCK, side='right'),
        N_EXPERTS - 1)
    xs = xt[buf_tok].reshape(n_blocks, EXPERT_BLOCK, dm)

    def expert_block(args):
        xb, e = args
        g = xb @ w_gate[e] + b_gate[e]
        u = xb @ w_up[e] + b_up[e]
        g = jnp.minimum(g, SWIGLU_LIMIT)
        u = jnp.clip(u, -SWIGLU_LIMIT, SWIGLU_LIMIT)
        hdn = (u + 1.0) * g * jax.nn.sigmoid(SWIGLU_ALPHA * g)
        return hdn @ w_down[e] + b_down[e]

    ys = lax.map(expert_block, (xs, blk_expert)).reshape(n_rows, dm)
    ys = ys.astype(jnp.float32) * buf_g[:, None]
    out = jnp.zeros((n_tok, dm), jnp.float32).at[buf_tok].add(ys)
    return out.astype(x.dtype).reshape(b, s, dm)


def setup_inputs(seed: int = 0) -> dict:
    key = jax.random.key(seed)
    ks = jax.random.split(key, 24)

    def nrm(k, shape, scale):
        return jax.random.normal(k, shape, jnp.float32) * scale

    L = DEPTH
    w_in = nrm(ks[2], (L, D_MODEL, D_IN_PROJ), D_MODEL ** -0.5)
    v_scale = jnp.concatenate([jnp.ones((D_IN_PROJ - D_ATTN,), jnp.float32),
                               jnp.full((D_ATTN,), DEEPNORM_BETA, jnp.float32)])
    w_in = w_in * v_scale
    positions = jnp.broadcast_to(jnp.arange(SEQ, dtype=jnp.int32), (BATCH, SEQ))
    return {
        "x": nrm(ks[0], (BATCH, SEQ, D_MODEL), 1.0),
        "positions": positions,
        "w_in": w_in,
        "conv_w": nrm(ks[3], (L, CONV_KERNEL, D_CONV), CONV_KERNEL ** -0.5),
        "conv_b": nrm(ks[4], (L, D_CONV), 0.02),
        "conv_ln_g": 1.0 + nrm(ks[5], (L, D_CONV), 0.02),
        "conv_ln_b": nrm(ks[6], (L, D_CONV), 0.02),
        "conv_pw_w": nrm(ks[7], (L, D_CONV, D_CONV), D_CONV ** -0.5 * DEEPNORM_BETA),
        "conv_pw_b": nrm(ks[8], (L, D_CONV), 0.02),
        "w_out": nrm(ks[9], (L, D_MIX, D_MODEL), D_MIX ** -0.5 * DEEPNORM_BETA),
        "ln1_g": 1.0 + nrm(ks[10], (L, D_MODEL), 0.02),
        "ln1_b": nrm(ks[11], (L, D_MODEL), 0.02),
        "router_w": nrm(ks[12], (L, D_MODEL, N_EXPERTS), D_MODEL ** -0.5),
        "router_b": nrm(ks[13], (L, N_EXPERTS), 0.01),
        "exp_w_gate": nrm(ks[14], (L, N_EXPERTS, D_MODEL, D_FF), D_MODEL ** -0.5),
        "exp_b_gate": nrm(ks[15], (L, N_EXPERTS, D_FF), 0.02),
        "exp_w_up": nrm(ks[16], (L, N_EXPERTS, D_MODEL, D_FF), D_MODEL ** -0.5),
        "exp_b_up": nrm(ks[17], (L, N_EXPERTS, D_FF), 0.02),
        "exp_w_down": nrm(ks[18], (L, N_EXPERTS, D_FF, D_MODEL), D_FF ** -0.5 * DEEPNORM_BETA),
        "exp_b_down": nrm(ks[19], (L, N_EXPERTS, D_MODEL), 0.02),
        "ln2_g": 1.0 + nrm(ks[20], (L, D_MODEL), 0.02),
        "ln2_b": nrm(ks[21], (L, D_MODEL), 0.02),
    }


def reference(x, positions, w_in, conv_w, conv_b, conv_ln_g, conv_ln_b, conv_pw_w,
              conv_pw_b, w_out, ln1_g, ln1_b, router_w, router_b, exp_w_gate,
              exp_b_gate, exp_w_up, exp_b_up, exp_w_down, exp_b_down, ln2_g, ln2_b):
    for l in range(DEPTH):
        mix = hybrid_mixer(x, positions, w_in[l], conv_w[l], conv_b[l], conv_ln_g[l],
                           conv_ln_b[l], conv_pw_w[l], conv_pw_b[l], w_out[l])
        x = layer_norm(DEEPNORM_ALPHA * x + mix, ln1_g[l], ln1_b[l])
        ffn = moe_ffn(x, router_w[l], router_b[l], exp_w_gate[l], exp_b_gate[l],
                      exp_w_up[l], exp_b_up[l], exp_w_down[l], exp_b_down[l])
        x = layer_norm(DEEPNORM_ALPHA * x + ffn, ln2_g[l], ln2_b[l])
    return x
```

```python
import functools

import jax
import jax.numpy as jnp
import numpy as np
from jax import lax
from jax.experimental import pallas as pl
from jax.experimental.pallas import tpu as pltpu

D_MODEL = 1024
D_CONV = 512
D_ATTN = 512
HEAD_DIM = 64
N_HEADS = D_ATTN // HEAD_DIM
CONV_KERNEL = 31
DILATED_BRANCHES = ((128, 1), (512, 4), (2048, 16))
ROPE_THETA = 10000.0
N_EXPERTS = 32
TOP_K = 4
D_FF = D_MODEL
SWIGLU_LIMIT = 7.0
SWIGLU_ALPHA = 1.702
LN_EPS = 1e-5
DEPTH = 1
DEEPNORM_ALPHA = float((2 * DEPTH) ** 0.25)

LANES = 128
SUBLANES = 8
HEADS_PER_GROUP = LANES // HEAD_DIM
N_GROUPS = N_HEADS // HEADS_PER_GROUP
WIN = 128
DILATIONS = tuple(d for _, d in DILATED_BRANCHES)
assert all(w // d == WIN for w, d in DILATED_BRANCHES)
UNITS = max(DILATIONS)
ATT_TILE = WIN * UNITS

TM_PROJ = 512
TS_CONV = 512
CONV_HALO = 32
EXPERT_ROWS = 256
TC_COMBINE = 256
TD_DISPATCH = 2048
VMEM_LIMIT = 56 * 1024 * 1024

_f32 = jnp.float32
_bf16 = jnp.bfloat16


def _layer_norm(h, g, b):
    mu = jnp.mean(h, axis=-1, keepdims=True)
    c = h - mu
    var = jnp.mean(c * c, axis=-1, keepdims=True)
    return c * lax.rsqrt(var + LN_EPS) * g + b


def _inproj_body(x_ref, pos_ref, invf_ref, w_ref, u_ref, q_ref, k_ref, v_ref):
    xb = x_ref[...].astype(_bf16)

    def proj(lo, hi):
        return jnp.dot(xb, w_ref[:, lo:hi], preferred_element_type=_f32)

    a = proj(0, D_CONV)
    gt = proj(D_CONV, 2 * D_CONV)
    u_ref[...] = a * jax.nn.sigmoid(gt)

    ang = pos_ref[...].astype(_f32) * invf_ref[...]
    cos = jnp.cos(ang)
    sin = jnp.sin(ang)
    lane = lax.broadcasted_iota(jnp.int32, (1, LANES), 1)
    first_half = (lane % HEAD_DIM) < (HEAD_DIM // 2)
    sin_signed = jnp.where(first_half, -sin, sin)

    def rope(t):
        rot = jnp.where(first_half, pltpu.roll(t, LANES - HEAD_DIM // 2, 1),
                        pltpu.roll(t, HEAD_DIM // 2, 1))
        return t * cos + rot * sin_signed

    q0 = 2 * D_CONV
    scale = 1.0 / np.sqrt(HEAD_DIM)
    for g in range(N_GROUPS):
        lo = g * LANES
        q = proj(q0 + lo, q0 + lo + LANES)
        q_ref[0, g] = (rope(q) * scale).astype(_bf16)
        k = proj(q0 + D_ATTN + lo, q0 + D_ATTN + lo + LANES)
        k_ref[0, g] = rope(k).astype(_bf16)
        v = proj(q0 + 2 * D_ATTN + lo, q0 + 2 * D_ATTN + lo + LANES)
        v_ref[0, g] = v.astype(_bf16)


def _inproj(x2d, pos2d, inv_freq, w_in_b, batch, seq):
    t = x2d.shape[0]
    tiles_per_seq = seq // TM_PROJ
    qkv_shape = jax.ShapeDtypeStruct((batch, N_GROUPS, seq, LANES), _bf16)
    qkv_spec = pl.BlockSpec((1, N_GROUPS, TM_PROJ, LANES),
                            lambda i: (i // tiles_per_seq, 0, i % tiles_per_seq, 0))
    return pl.pallas_call(
        _inproj_body,
        grid=(t // TM_PROJ,),
        in_specs=[
            pl.BlockSpec((TM_PROJ, D_MODEL), lambda i: (i, 0)),
            pl.BlockSpec((TM_PROJ, 1), lambda i: (i, 0)),
            pl.BlockSpec((1, LANES), lambda i: (0, 0)),
            pl.BlockSpec(w_in_b.shape, lambda i: (0, 0)),
        ],
        out_specs=[pl.BlockSpec((TM_PROJ, D_CONV), lambda i: (i, 0)), qkv_spec, qkv_spec, qkv_spec],
        out_shape=[jax.ShapeDtypeStruct((t, D_CONV), _f32), qkv_shape, qkv_shape, qkv_shape],
        compiler_params=pltpu.CompilerParams(dimension_semantics=("arbitrary",),
                                             vmem_limit_bytes=VMEM_LIMIT),
        name="inproj",
    )(x2d, pos2d, inv_freq, w_in_b)


def _conv_body(u_ref, halo_ref, cw_ref, cb_ref, g_ref, b_ref, pw_ref, pb_ref, o_ref, buf_ref):
    i = pl.program_id(1)
    halo = halo_ref[0]
    buf_ref[0:CONV_HALO] = jnp.where(i > 0, halo, jnp.zeros_like(halo))
    buf_ref[CONV_HALO:CONV_HALO + TS_CONV] = u_ref[0]
    acc = jnp.broadcast_to(cb_ref[...], (TS_CONV, D_CONV))
    first = CONV_HALO - (CONV_KERNEL - 1)
    for j in range(CONV_KERNEL):
        acc = acc + cw_ref[j:j + 1, :] * buf_ref[first + j:first + j + TS_CONV]
    y = _layer_norm(acc, g_ref[...], b_ref[...])
    y = y * jax.nn.sigmoid(y)
    o = jnp.dot(y.astype(_bf16), pw_ref[...], preferred_element_type=_f32) + pb_ref[...]
    o_ref[0] = o.astype(_bf16)


def _conv_module(u3d, conv_w, conv_b, ln_g, ln_b, pw_b16, pw_bias):
    batch, seq, _ = u3d.shape
    halo_blocks = TS_CONV // CONV_HALO
    vec = lambda a: a.reshape(1, -1)
    const = lambda shape: pl.BlockSpec(shape, lambda b, i: (0, 0))
    return pl.pallas_call(
        _conv_body,
        grid=(batch, seq // TS_CONV),
        in_specs=[
            pl.BlockSpec((1, TS_CONV, D_CONV), lambda b, i: (b, i, 0)),
            pl.BlockSpec((1, CONV_HALO, D_CONV),
                         lambda b, i: (b, jnp.maximum(i * halo_blocks - 1, 0), 0)),
            const((CONV_KERNEL, D_CONV)), const((1, D_CONV)), const((1, D_CONV)), const((1, D_CONV)),
            const((D_CONV, D_CONV)), const((1, D_CONV)),
        ],
        out_specs=pl.BlockSpec((1, TS_CONV, D_CONV), lambda b, i: (b, i, 0)),
        out_shape=jax.ShapeDtypeStruct((batch, seq, D_CONV), _bf16),
        scratch_shapes=[pltpu.VMEM((CONV_HALO + TS_CONV, D_CONV), _f32)],
        compiler_params=pltpu.CompilerParams(dimension_semantics=("arbitrary", "arbitrary"),
                                             vmem_limit_bytes=VMEM_LIMIT),
        name="conv",
    )(u3d, u3d, conv_w, vec(conv_b), vec(ln_g), vec(ln_b), pw_b16, vec(pw_bias))


def _attn_unit(q, kk, vv, prev_valid):
    lane = lax.broadcasted_iota(jnp.int32, (WIN, LANES), 1)
    head0 = lane < HEAD_DIM
    zero = jnp.zeros_like(q)
    qq = jnp.concatenate([jnp.where(head0, q, zero), jnp.where(head0, zero, q)], axis=0)
    s = lax.dot_general(qq, kk, (((1,), (1,)), ((), ())), preferred_element_type=_f32)
    qi = lax.broadcasted_iota(jnp.int32, (2 * WIN, 2 * WIN), 0) % WIN
    ki = lax.broadcasted_iota(jnp.int32, (2 * WIN, 2 * WIN), 1)
    rel = qi + WIN - ki
    mask = (rel >= 0) & (rel <= WIN) & ((ki >= WIN) | prev_valid)
    s = jnp.where(mask, s, -jnp.inf)
    m = jnp.max(s, axis=-1, keepdims=True)
    p = jnp.exp(s - m)
    den = jnp.sum(p, axis=-1, keepdims=True)
    acc = jnp.dot(p.astype(_bf16), vv, preferred_element_type=_f32)
    pick = lambda a: jnp.where(head0, a[:WIN], a[WIN:])
    return (pick(acc), pick(jnp.broadcast_to(m, (2 * WIN, LANES))),
            pick(jnp.broadcast_to(den, (2 * WIN, LANES))))


def _attn_body(q1, q4, q16, k1c, k1p, k4c, k4p, k16c, k16p, v1c, v1p, v4c, v4p, v16c, v16p,
               o_ref, qs, ks, vs, acc_s, m_s, den_s):
    j = pl.program_id(2)
    not_first_tile = j > 0

    q_in = (q1, q4, q16)
    kv_in = (((k1c, k1p, ks), (v1c, v1p, vs)), ((k4c, k4p, ks), (v4c, v4p, vs)),
             ((k16c, k16p, ks), (v16c, v16p, vs)))
    k_base = []
    slot = 0
    for bi, d in enumerate(DILATIONS):
        nblk = UNITS // d
        k_base.append(slot)
        for r in range(d):
            lanes = slice(r * LANES, (r + 1) * LANES)
            for cur, prev, dst in kv_in[bi]:
                dst[slot] = prev[:, lanes]
                for n in range(nblk):
                    dst[slot + 1 + n] = cur[n * WIN:(n + 1) * WIN, lanes]
            for n in range(nblk):
                qs[bi * UNITS + r * nblk + n] = q_in[bi][n * WIN:(n + 1) * WIN, lanes]
            slot += nblk + 1

    for bi, d in enumerate(DILATIONS):
        nblk = UNITS // d

        def unit(u, carry, bi=bi, d=d, nblk=nblk):
            r = u // nblk
            n = u % nblk
            kslot = k_base[bi] + r * (nblk + 1) + n
            kk = ks[pl.ds(kslot, 2)].reshape(2 * WIN, LANES)
            vv = vs[pl.ds(kslot, 2)].reshape(2 * WIN, LANES)
            prev_valid = jnp.logical_or(n > 0, not_first_tile)
            acc, m, den = _attn_unit(qs[bi * UNITS + u], kk, vv, prev_valid)
            start = n * (WIN * d) + r
            rows = pl.ds(start, WIN, stride=d) if d > 1 else pl.ds(pl.multiple_of(start, WIN), WIN)
            acc_s[bi, rows, :] = acc
            m_s[bi, rows, :] = m
            den_s[bi, rows, :] = den
            return carry

        lax.fori_loop(0, UNITS, unit, 0)

    def merge(c, carry):
        rows = pl.ds(pl.multiple_of(c * WIN, WIN), WIN)
        ms = [m_s[bi, rows, :] for bi in range(len(DILATIONS))]
        mm = functools.reduce(jnp.maximum, ms)
        ws = [jnp.exp(m - mm) for m in ms]
        num = sum(w * acc_s[bi, rows, :] for bi, w in enumerate(ws))
        den = sum(w * den_s[bi, rows, :] for bi, w in enumerate(ws))
        o_ref[rows, :] = (num / den).astype(o_ref.dtype)
        return carry

    lax.fori_loop(0, UNITS, merge, 0)


def _attention(q, k, v):
    batch, groups, seq, _ = q.shape
    n_tiles = seq // ATT_TILE

    def views(a):
        return [a.reshape(batch, groups, seq // d, d * LANES) for d in DILATIONS]

    def cur_spec(d):
        return pl.BlockSpec((None, None, ATT_TILE // d, d * LANES), lambda b, g, j: (b, g, j, 0))

    def prev_spec(d):
        per_tile = ATT_TILE // d // WIN
        return pl.BlockSpec((None, None, WIN, d * LANES),
                            lambda b, g, j: (b, g, jnp.maximum(j * per_tile - 1, 0), 0))

    q_views = views(q)
    k_views = views(k)
    v_views = views(v)
    args = list(q_views)
    specs = [cur_spec(d) for d in DILATIONS]
    for vw in (k_views, v_views):
        for a, d in zip(vw, DILATIONS):
            args += [a, a]
            specs += [cur_spec(d), prev_spec(d)]
    n_kslots = sum(d * (UNITS // d + 1) for d in DILATIONS)
    nb = len(DILATIONS)
    return pl.pallas_call(
        _attn_body,
        grid=(batch, groups, n_tiles),
        in_specs=specs,
        out_specs=pl.BlockSpec((None, ATT_TILE, LANES), lambda b, g, j: (b, j, g)),
        out_shape=jax.ShapeDtypeStruct((batch, seq, D_ATTN), _bf16),
        scratch_shapes=[
            pltpu.VMEM((nb * UNITS, WIN, LANES), _bf16),
            pltpu.VMEM((n_kslots, WIN, LANES), _bf16),
            pltpu.VMEM((n_kslots, WIN, LANES), _bf16),
            pltpu.VMEM((nb, ATT_TILE, LANES), _f32),
            pltpu.VMEM((nb, ATT_TILE, LANES), _f32),
            pltpu.VMEM((nb, ATT_TILE, LANES), _f32),
        ],
        compiler_params=pltpu.CompilerParams(
            dimension_semantics=("arbitrary", "arbitrary", "arbitrary"),
            vmem_limit_bytes=VMEM_LIMIT),
        name="attn",
    )(*args)


def _outproj_body(conv_ref, attn_ref, x_ref, wo_ref, g_ref, b_ref, rwt_ref, rb_ref, tri_ref,
                  x1_ref, idx_ref, gate_ref, rank_ref, cnt_ref, carry_ref):
    i = pl.program_id(0)

    @pl.when(i == 0)
    def _():
        carry_ref[...] = jnp.zeros_like(carry_ref)

    mix = jnp.dot(conv_ref[...], wo_ref[0:D_CONV, :], preferred_element_type=_f32)
    mix = mix + jnp.dot(attn_ref[...], wo_ref[D_CONV:, :], preferred_element_type=_f32)
    x1 = _layer_norm(DEEPNORM_ALPHA * x_ref[...] + mix, g_ref[...], b_ref[...])
    x1_ref[...] = x1

    logits = lax.dot_general(rwt_ref[...], x1, (((1,), (1,)), ((), ())),
                             precision=lax.Precision.HIGHEST,
                             preferred_element_type=_f32) + rb_ref[...]
    e_iota = lax.broadcasted_iota(jnp.int32, logits.shape, 0)
    work = logits
    tops, sels = [], []
    for k in range(TOP_K):
        mx = jnp.max(work, axis=0, keepdims=True)
        idx = jnp.min(jnp.where(work == mx, e_iota, N_EXPERTS), axis=0, keepdims=True)
        sel = e_iota == idx
        work = jnp.where(sel, -jnp.inf, work)
        tops.append(mx)
        sels.append(sel)
        idx_ref[k:k + 1, :] = idx
    exps = [jnp.exp(t - tops[0]) for t in tops]
    total = functools.reduce(jnp.add, exps)
    for k in range(TOP_K):
        gate_ref[k:k + 1, :] = exps[k] / total

    chosen = functools.reduce(jnp.logical_or, sels)
    chosen_f = chosen.astype(_f32)
    before = jnp.dot(chosen_f.astype(_bf16), tri_ref[...], preferred_element_type=_f32)
    base = carry_ref[:, 0:1] + before
    for k in range(TOP_K):
        rank = jnp.sum(jnp.where(sels[k], base, 0.0), axis=0, keepdims=True)
        rank_ref[k:k + 1, :] = rank.astype(jnp.int32)
    carry_ref[...] = carry_ref[...] + jnp.sum(chosen_f, axis=1, keepdims=True)
    cnt_ref[...] = carry_ref[...].astype(jnp.int32)


def _outproj(conv2d, attn2d, x2d, w_out_b, ln_g, ln_b, router_wt, router_b):
    t = x2d.shape[0]
    tm = TM_PROJ
    tri = (jnp.arange(tm)[:, None] < jnp.arange(tm)[None, :]).astype(_bf16)
    const = lambda shape: pl.BlockSpec(shape, lambda i: (0, 0))
    tok = lambda rows: pl.BlockSpec((rows, tm), lambda i: (0, i))
    return pl.pallas_call(
        _outproj_body,
        grid=(t // tm,),
        in_specs=[
            pl.BlockSpec((tm, D_CONV), lambda i: (i, 0)),
            pl.BlockSpec((tm, D_ATTN), lambda i: (i, 0)),
            pl.BlockSpec((tm, D_MODEL), lambda i: (i, 0)),
            const((D_MODEL, D_MODEL)), const((1, D_MODEL)), const((1, D_MODEL)),
            const((N_EXPERTS, D_MODEL)), const((N_EXPERTS, 1)), const((tm, tm)),
        ],
        out_specs=[pl.BlockSpec((tm, D_MODEL), lambda i: (i, 0)),
                   tok(TOP_K), tok(TOP_K), tok(TOP_K), const((N_EXPERTS, LANES))],
        out_shape=[jax.ShapeDtypeStruct((t, D_MODEL), _f32),
                   jax.ShapeDtypeStruct((TOP_K, t), jnp.int32),
                   jax.ShapeDtypeStruct((TOP_K, t), _f32),
                   jax.ShapeDtypeStruct((TOP_K, t), jnp.int32),
                   jax.ShapeDtypeStruct((N_EXPERTS, LANES), jnp.int32)],
        scratch_shapes=[pltpu.VMEM((N_EXPERTS, LANES), _f32)],
        compiler_params=pltpu.CompilerParams(dimension_semantics=("arbitrary",),
                                             vmem_limit_bytes=VMEM_LIMIT),
        name="outproj",
    )(conv2d, attn2d, x2d, w_out_b, ln_g.reshape(1, -1), ln_b.reshape(1, -1),
      router_wt, router_b.reshape(-1, 1), tri)


def _dispatch_body(tail_ref, dest_ref, x1_hbm, xs_hbm, zeros_ref, sem):
    i = pl.program_id(0)

    @pl.when(i == 0)
    def _():
        zeros_ref[...] = jnp.zeros_like(zeros_ref)

        def tail_copy(e):
            return pltpu.make_async_copy(
                zeros_ref, xs_hbm.at[pl.ds(pl.multiple_of(tail_ref[e], SUBLANES), EXPERT_ROWS)], sem)

        def start(e, c):
            tail_copy(e).start()
            return c

        def wait(e, c):
            tail_copy(e).wait()
            return c

        lax.fori_loop(0, N_EXPERTS, start, 0)
        lax.fori_loop(0, N_EXPERTS, wait, 0)

    def row_copy(t, k):
        return pltpu.make_async_copy(x1_hbm.at[pl.ds(i * TD_DISPATCH + t, 1)],
                                     xs_hbm.at[pl.ds(dest_ref[k, t], 1)], sem)

    def start(t, c):
        for k in range(TOP_K):
            row_copy(t, k).start()
        return c

    def wait(t, c):
        for k in range(TOP_K):
            row_copy(t, k).wait()
        return c

    lax.fori_loop(0, TD_DISPATCH, start, 0)
    lax.fori_loop(0, TD_DISPATCH, wait, 0)


def _dispatch(x1, dest, tail_start, n_rows):
    t = x1.shape[0]
    return pl.pallas_call(
        _dispatch_body,
        grid_spec=pltpu.PrefetchScalarGridSpec(
            num_scalar_prefetch=1,
            grid=(t // TD_DISPATCH,),
            in_specs=[
                pl.BlockSpec((TOP_K, TD_DISPATCH), lambda i, tail: (0, i), memory_space=pltpu.SMEM),
                pl.BlockSpec(memory_space=pl.ANY),
            ],
            out_specs=pl.BlockSpec(memory_space=pl.ANY),
            scratch_shapes=[pltpu.VMEM((EXPERT_ROWS, D_MODEL), _f32), pltpu.SemaphoreType.DMA(())],
        ),
        out_shape=jax.ShapeDtypeStruct((n_rows, D_MODEL), _f32),
        compiler_params=pltpu.CompilerParams(dimension_semantics=("arbitrary",),
                                             vmem_limit_bytes=VMEM_LIMIT),
        name="dispatch",
    )(tail_start, dest, x1)


def _expert_body(blk_e_ref, nused_ref, xs_ref, wg_ref, bg_ref, wu_ref, bu_ref, wd_ref, bd_ref,
                 ys_ref, wg_b, wu_b, wd_b):
    n = pl.program_id(0)
    active = n < nused_ref[0]
    prev_e = blk_e_ref[jnp.maximum(n - 1, 0)]
    new_expert = jnp.logical_or(n == 0, blk_e_ref[n] != prev_e)

    @pl.when(jnp.logical_and(active, new_expert))
    def _():
        wg_b[...] = wg_ref[0].astype(_bf16)
        wu_b[...] = wu_ref[0].astype(_bf16)
        wd_b[...] = wd_ref[0].astype(_bf16)

    @pl.when(active)
    def _():
        xb = xs_ref[...].astype(_bf16)
        g = jnp.dot(xb, wg_b[...], preferred_element_type=_f32) + bg_ref[0]
        u = jnp.dot(xb, wu_b[...], preferred_element_type=_f32) + bu_ref[0]
        g = jnp.minimum(g, SWIGLU_LIMIT)
        u = jnp.clip(u, -SWIGLU_LIMIT, SWIGLU_LIMIT)
        hdn = (u + 1.0) * g * jax.nn.sigmoid(SWIGLU_ALPHA * g)
        ys_ref[...] = jnp.dot(hdn.astype(_bf16), wd_b[...], preferred_element_type=_f32) + bd_ref[0]


def _experts(xs, blk_expert, n_used, w_gate, b_gate, w_up, b_up, w_down, b_down):
    n_rows = xs.shape[0]
    n_blocks = n_rows // EXPERT_ROWS

    def row_map(n, blk_e, nused):
        return (jnp.minimum(n, nused[0] - 1), 0)

    def w_map(n, blk_e, nused):
        return (blk_e[jnp.minimum(n, nused[0] - 1)], 0, 0)

    w_spec = pl.BlockSpec((1, D_MODEL, D_FF), w_map)
    b_spec = pl.BlockSpec((1, 1, D_FF), w_map)
    bias = lambda b: b.reshape(N_EXPERTS, 1, -1)
    return pl.pallas_call(
        _expert_body,
        grid_spec=pltpu.PrefetchScalarGridSpec(
            num_scalar_prefetch=2,
            grid=(n_blocks,),
            in_specs=[pl.BlockSpec((EXPERT_ROWS, D_MODEL), row_map),
                      w_spec, b_spec, w_spec, b_spec, w_spec, b_spec],
            out_specs=pl.BlockSpec((EXPERT_ROWS, D_MODEL), row_map),
            scratch_shapes=[pltpu.VMEM((D_MODEL, D_FF), _bf16)] * 3,
        ),
        out_shape=jax.ShapeDtypeStruct((n_rows, D_MODEL), _f32),
        compiler_params=pltpu.CompilerParams(dimension_semantics=("arbitrary",),
                                             vmem_limit_bytes=VMEM_LIMIT),
        name="experts",
    )(blk_expert, n_used, xs, w_gate, bias(b_gate), w_up, bias(b_up), w_down, bias(b_down))


def _combine_body(dest_ref, gate_ref, x1_ref, g_ref, b_ref, ys_hbm, o_ref, buf_ref, sem):
    def row_copy(t, k):
        return pltpu.make_async_copy(ys_hbm.at[pl.ds(dest_ref[k, t], 1)],
                                     buf_ref.at[k, pl.ds(t, 1)], sem)

    def start(t, c):
        for k in range(TOP_K):
            row_copy(t, k).start()
        return c

    def wait(t, c):
        for k in range(TOP_K):
            row_copy(t, k).wait()
        return c

    lax.fori_loop(0, TC_COMBINE, start, 0)
    lax.fori_loop(0, TC_COMBINE, wait, 0)

    ffn = gate_ref[:, 0:1] * buf_ref[0]
    for k in range(1, TOP_K):
        ffn = ffn + gate_ref[:, k:k + 1] * buf_ref[k]
    o_ref[...] = _layer_norm(DEEPNORM_ALPHA * x1_ref[...] + ffn, g_ref[...], b_ref[...])


def _combine(dest, gate_tk, x1, ln_g, ln_b, ys):
    t = x1.shape[0]
    tc = TC_COMBINE
    return pl.pallas_call(
        _combine_body,
        grid=(t // tc,),
        in_specs=[
            pl.BlockSpec((TOP_K, tc), lambda i: (0, i), memory_space=pltpu.SMEM),
            pl.BlockSpec((tc, TOP_K), lambda i: (i, 0)),
            pl.BlockSpec((tc, D_MODEL), lambda i: (i, 0)),
            pl.BlockSpec((1, D_MODEL), lambda i: (0, 0)),
            pl.BlockSpec((1, D_MODEL), lambda i: (0, 0)),
            pl.BlockSpec(memory_space=pl.ANY),
        ],
        out_specs=pl.BlockSpec((tc, D_MODEL), lambda i: (i, 0)),
        out_shape=jax.ShapeDtypeStruct((t, D_MODEL), _f32),
        scratch_shapes=[pltpu.VMEM((TOP_K, tc, D_MODEL), _f32), pltpu.SemaphoreType.DMA(())],
        compiler_params=pltpu.CompilerParams(dimension_semantics=("arbitrary",),
                                             vmem_limit_bytes=VMEM_LIMIT),
        name="combine",
    )(dest, gate_tk, x1, ln_g.reshape(1, -1), ln_b.reshape(1, -1), ys)


def kernel(x, positions, w_in, conv_w, conv_b, conv_ln_g, conv_ln_b, conv_pw_w, conv_pw_b, w_out,
           ln1_g, ln1_b, router_w, router_b, exp_w_gate, exp_b_gate, exp_w_up, exp_b_up,
           exp_w_down, exp_b_down, ln2_g, ln2_b):
    batch, seq, _ = x.shape
    t = batch * seq
    for l in range(DEPTH):
        x2d = x.reshape(t, D_MODEL)
        half = jnp.arange(0, HEAD_DIM, 2, dtype=_f32)
        inv_freq = ROPE_THETA ** (-half / HEAD_DIM)
        inv_freq = jnp.tile(inv_freq, 2 * HEADS_PER_GROUP).reshape(1, LANES)

        u, q, k, v = _inproj(x2d, positions.reshape(t, 1), inv_freq, w_in[l].astype(_bf16), batch, seq)
        conv_out = _conv_module(u.reshape(batch, seq, D_CONV), conv_w[l], conv_b[l], conv_ln_g[l],
                                conv_ln_b[l], conv_pw_w[l].astype(_bf16), conv_pw_b[l])
        attn = _attention(q, k, v)
        x1, top_idx, gate, rank, counts = _outproj(
            conv_out.reshape(t, D_CONV), attn.reshape(t, D_ATTN), x2d, w_out[l].astype(_bf16),
            ln1_g[l], ln1_b[l], router_w[l].T, router_b[l])

        counts = counts[:, 0]
        padded = (counts + EXPERT_ROWS - 1) // EXPERT_ROWS * EXPERT_ROWS
        pends = jnp.cumsum(padded)
        pstarts = pends - padded
        n_blocks = (t * TOP_K) // EXPERT_ROWS + N_EXPERTS
        n_rows = n_blocks * EXPERT_ROWS
        dest = pstarts[top_idx] + rank
        tail_start = jnp.minimum(pstarts + counts // SUBLANES * SUBLANES,
                                 n_rows - EXPERT_ROWS).astype(jnp.int32)
        blk_expert = jnp.minimum(
            jnp.searchsorted(pends, jnp.arange(n_blocks, dtype=jnp.int32) * EXPERT_ROWS, side='right'),
            N_EXPERTS - 1).astype(jnp.int32)
        n_used = (pends[-1] // EXPERT_ROWS).astype(jnp.int32).reshape(1)

        xs = _dispatch(x1, dest, tail_start, n_rows)
        ys = _experts(xs, blk_expert, n_used, exp_w_gate[l], exp_b_gate[l], exp_w_up[l], exp_b_up[l],
                      exp_w_down[l], exp_b_down[l])
        out = _combine(dest, gate.T, x1, ln2_g[l], ln2_b[l], ys)
        x = out.reshape(batch, seq, D_MODEL)
    return x
```

```python
import functools

import jax
import jax.numpy as jnp
import numpy as np
from jax import lax
from jax.experimental import pallas as pl
from jax.experimental.pallas import tpu as pltpu

D_MODEL = 1024
D_CONV = 512
D_ATTN = 512
HEAD_DIM = 64
N_HEADS = D_ATTN // HEAD_DIM
CONV_KERNEL = 31
DILATED_BRANCHES = ((128, 1), (512, 4), (2048, 16))
ROPE_THETA = 10000.0
N_EXPERTS = 32
TOP_K = 4
D_FF = D_MODEL
SWIGLU_LIMIT = 7.0
SWIGLU_ALPHA = 1.702
LN_EPS = 1e-5
DEPTH = 1
DEEPNORM_ALPHA = float((2 * DEPTH) ** 0.25)

LANES = 128
SUBLANES = 8
HEADS_PER_GROUP = LANES // HEAD_DIM
N_GROUPS = N_HEADS // HEADS_PER_GROUP
WIN = 128
DILATIONS = tuple(d for _, d in DILATED_BRANCHES)
assert all(w // d == WIN for w, d in DILATED_BRANCHES)
UNITS = max(DILATIONS)
ATT_TILE = WIN * UNITS

TM_PROJ = 512
TS_CONV = 512
CONV_HALO = 32
EXPERT_ROWS = 256
TC_COMBINE = 256
TD_DISPATCH = 512
ATT_UNROLL = 2
ROW_DMA_UNROLL = 8
VMEM_LIMIT = 56 * 1024 * 1024

_f32 = jnp.float32
_bf16 = jnp.bfloat16


def _layer_norm(h, g, b):
    mu = jnp.mean(h, axis=-1, keepdims=True)
    c = h - mu
    var = jnp.mean(c * c, axis=-1, keepdims=True)
    return c * lax.rsqrt(var + LN_EPS) * g + b


def _inproj_body(x_ref, pos_ref, invf_ref, w_ref, u_ref, q_ref, k_ref, v_ref):
    xb = x_ref[...].astype(_bf16)

    def proj(lo, hi):
        return jnp.dot(xb, w_ref[:, lo:hi], preferred_element_type=_f32)

    a = proj(0, D_CONV)
    gt = proj(D_CONV, 2 * D_CONV)
    u_ref[...] = a * jax.nn.sigmoid(gt)

    ang = pos_ref[...].astype(_f32) * invf_ref[...]
    cos = jnp.cos(ang)
    sin = jnp.sin(ang)
    lane = lax.broadcasted_iota(jnp.int32, (1, LANES), 1)
    first_half = (lane % HEAD_DIM) < (HEAD_DIM // 2)
    sin_signed = jnp.where(first_half, -sin, sin)

    def rope(t):
        rot = jnp.where(first_half, pltpu.roll(t, LANES - HEAD_DIM // 2, 1),
                        pltpu.roll(t, HEAD_DIM // 2, 1))
        return t * cos + rot * sin_signed

    q0 = 2 * D_CONV
    scale = 1.0 / np.sqrt(HEAD_DIM)
    for g in range(N_GROUPS):
        lo = g * LANES
        q = proj(q0 + lo, q0 + lo + LANES)
        q_ref[0, g] = rope(q) * scale
        k = proj(q0 + D_ATTN + lo, q0 + D_ATTN + lo + LANES)
        k_ref[0, g] = rope(k)
        v = proj(q0 + 2 * D_ATTN + lo, q0 + 2 * D_ATTN + lo + LANES)
        v_ref[0, g] = v


def _inproj(x2d, pos2d, inv_freq, w_in_b, batch, seq):
    t = x2d.shape[0]
    tiles_per_seq = seq // TM_PROJ
    qkv_shape = jax.ShapeDtypeStruct((batch, N_GROUPS, seq, LANES), _f32)
    qkv_spec = pl.BlockSpec((1, N_GROUPS, TM_PROJ, LANES),
                            lambda i: (i // tiles_per_seq, 0, i % tiles_per_seq, 0))
    return pl.pallas_call(
        _inproj_body,
        grid=(t // TM_PROJ,),
        in_specs=[
            pl.BlockSpec((TM_PROJ, D_MODEL), lambda i: (i, 0)),
            pl.BlockSpec((TM_PROJ, 1), lambda i: (i, 0)),
            pl.BlockSpec((1, LANES), lambda i: (0, 0)),
            pl.BlockSpec(w_in_b.shape, lambda i: (0, 0)),
        ],
        out_specs=[pl.BlockSpec((TM_PROJ, D_CONV), lambda i: (i, 0)), qkv_spec, qkv_spec, qkv_spec],
        out_shape=[jax.ShapeDtypeStruct((t, D_CONV), _f32), qkv_shape, qkv_shape, qkv_shape],
        compiler_params=pltpu.CompilerParams(dimension_semantics=("arbitrary",),
                                             vmem_limit_bytes=VMEM_LIMIT),
        name="inproj",
    )(x2d, pos2d, inv_freq, w_in_b)


def _conv_body(u_ref, halo_ref, cw_ref, cb_ref, g_ref, b_ref, pw_ref, pb_ref, o_ref, buf_ref):
    i = pl.program_id(1)
    halo = halo_ref[0]
    buf_ref[0:CONV_HALO] = jnp.where(i > 0, halo, jnp.zeros_like(halo))
    buf_ref[CONV_HALO:CONV_HALO + TS_CONV] = u_ref[0]
    acc = jnp.broadcast_to(cb_ref[...], (TS_CONV, D_CONV))
    first = CONV_HALO - (CONV_KERNEL - 1)
    for j in range(CONV_KERNEL):
        acc = acc + cw_ref[j:j + 1, :] * buf_ref[first + j:first + j + TS_CONV]
    y = _layer_norm(acc, g_ref[...], b_ref[...])
    y = y * jax.nn.sigmoid(y)
    o = jnp.dot(y.astype(_bf16), pw_ref[...], preferred_element_type=_f32) + pb_ref[...]
    o_ref[0] = o.astype(_bf16)


def _conv_module(u3d, conv_w, conv_b, ln_g, ln_b, pw_b16, pw_bias):
    batch, seq, _ = u3d.shape
    halo_blocks = TS_CONV // CONV_HALO
    vec = lambda a: a.reshape(1, -1)
    const = lambda shape: pl.BlockSpec(shape, lambda b, i: (0, 0))
    return pl.pallas_call(
        _conv_body,
        grid=(batch, seq // TS_CONV),
        in_specs=[
            pl.BlockSpec((1, TS_CONV, D_CONV), lambda b, i: (b, i, 0)),
            pl.BlockSpec((1, CONV_HALO, D_CONV),
                         lambda b, i: (b, jnp.maximum(i * halo_blocks - 1, 0), 0)),
            const((CONV_KERNEL, D_CONV)), const((1, D_CONV)), const((1, D_CONV)), const((1, D_CONV)),
            const((D_CONV, D_CONV)), const((1, D_CONV)),
        ],
        out_specs=pl.BlockSpec((1, TS_CONV, D_CONV), lambda b, i: (b, i, 0)),
        out_shape=jax.ShapeDtypeStruct((batch, seq, D_CONV), _bf16),
        scratch_shapes=[pltpu.VMEM((CONV_HALO + TS_CONV, D_CONV), _f32)],
        compiler_params=pltpu.CompilerParams(dimension_semantics=("arbitrary", "arbitrary"),
                                             vmem_limit_bytes=VMEM_LIMIT),
        name="conv",
    )(u3d, u3d, conv_w, vec(conv_b), vec(ln_g), vec(ln_b), pw_b16, vec(pw_bias))


def _attn_unit(q, kk, vv, prev_valid):
    lane = lax.broadcasted_iota(jnp.int32, (WIN, LANES), 1)
    head0 = lane < HEAD_DIM
    zero = jnp.zeros_like(q)
    qq = jnp.concatenate([jnp.where(head0, q, zero), jnp.where(head0, zero, q)], axis=0)
    s = lax.dot_general(qq, kk, (((1,), (1,)), ((), ())), preferred_element_type=_f32)
    qi = lax.broadcasted_iota(jnp.int32, (2 * WIN, 2 * WIN), 0) % WIN
    ki = lax.broadcasted_iota(jnp.int32, (2 * WIN, 2 * WIN), 1)
    rel = qi + WIN - ki
    mask = (rel >= 0) & (rel <= WIN) & ((ki >= WIN) | prev_valid)
    s = jnp.where(mask, s, -jnp.inf)
    m = jnp.max(s, axis=-1, keepdims=True)
    p = jnp.exp(s - m)
    den = jnp.sum(p, axis=-1, keepdims=True)
    acc = jnp.dot(p.astype(_bf16), vv, preferred_element_type=_f32)
    pick = lambda a: jnp.where(head0, a[:WIN], a[WIN:])
    return (pick(acc), pick(jnp.broadcast_to(m, (2 * WIN, LANES))),
            pick(jnp.broadcast_to(den, (2 * WIN, LANES))))


def _attn_body(q_ref, kc_ref, kp_ref, vc_ref, vp_ref, o_ref, qs, ks, vs, acc_s, m_s, den_s):
    j = pl.program_id(2)
    not_first_tile = j > 0

    def rows(start, d):
        return pl.ds(start, WIN, stride=d) if d > 1 else pl.ds(start, WIN)

    k_base = []
    slot = 0
    for bi, d in enumerate(DILATIONS):
        nblk = UNITS // d
        k_base.append(slot)
        for r in range(d):
            for cur, prev, dst in ((kc_ref, kp_ref, ks), (vc_ref, vp_ref, vs)):
                dst[slot] = prev[rows(ATT_TILE - WIN * d + r, d), :].astype(_bf16)
                for n in range(nblk):
                    dst[slot + 1 + n] = cur[rows(n * WIN * d + r, d), :].astype(_bf16)
            for n in range(nblk):
                qs[bi * UNITS + r * nblk + n] = q_ref[rows(n * WIN * d + r, d), :].astype(_bf16)
            slot += nblk + 1

    for bi, d in enumerate(DILATIONS):
        nblk = UNITS // d

        def units(it, carry, bi=bi, d=d, nblk=nblk):
            for sub in range(ATT_UNROLL):
                u = it * ATT_UNROLL + sub
                r = u // nblk
                n = u % nblk
                kslot = k_base[bi] + r * (nblk + 1) + n
                kk = ks[pl.ds(kslot, 2)].reshape(2 * WIN, LANES)
                vv = vs[pl.ds(kslot, 2)].reshape(2 * WIN, LANES)
                prev_valid = jnp.logical_or(n > 0, not_first_tile)
                acc, m, den = _attn_unit(qs[bi * UNITS + u], kk, vv, prev_valid)
                start = n * (WIN * d) + r
                out_rows = (pl.ds(start, WIN, stride=d) if d > 1
                            else pl.ds(pl.multiple_of(start, WIN), WIN))
                acc_s[bi, out_rows, :] = acc
                m_s[bi, out_rows, :] = m
                den_s[bi, out_rows, :] = den
            return carry

        lax.fori_loop(0, UNITS // ATT_UNROLL, units, 0)

    def merge(c, carry):
        rows_c = pl.ds(pl.multiple_of(c * WIN, WIN), WIN)
        ms = [m_s[bi, rows_c, :] for bi in range(len(DILATIONS))]
        mm = functools.reduce(jnp.maximum, ms)
        ws = [jnp.exp(m - mm) for m in ms]
        num = sum(w * acc_s[bi, rows_c, :] for bi, w in enumerate(ws))
        den = sum(w * den_s[bi, rows_c, :] for bi, w in enumerate(ws))
        o_ref[rows_c, :] = (num / den).astype(o_ref.dtype)
        return carry

    lax.fori_loop(0, UNITS, merge, 0)


def _attention(q, k, v):
    batch, groups, seq, _ = q.shape
    n_tiles = seq // ATT_TILE
    cur = pl.BlockSpec((None, None, ATT_TILE, LANES), lambda b, g, j: (b, g, j, 0))
    prev = pl.BlockSpec((None, None, ATT_TILE, LANES),
                        lambda b, g, j: (b, g, jnp.maximum(j - 1, 0), 0))
    args = (q, k, k, v, v)
    n_kslots = sum(d * (UNITS // d + 1) for d in DILATIONS)
    nb = len(DILATIONS)
    return pl.pallas_call(
        _attn_body,
        grid=(batch, groups, n_tiles),
        in_specs=[cur, cur, prev, cur, prev],
        out_specs=pl.BlockSpec((None, ATT_TILE, LANES), lambda b, g, j: (b, j, g)),
        out_shape=jax.ShapeDtypeStruct((batch, seq, D_ATTN), _bf16),
        scratch_shapes=[
            pltpu.VMEM((nb * UNITS, WIN, LANES), _bf16),
            pltpu.VMEM((n_kslots, WIN, LANES), _bf16),
            pltpu.VMEM((n_kslots, WIN, LANES), _bf16),
            pltpu.VMEM((nb, ATT_TILE, LANES), _f32),
            pltpu.VMEM((nb, ATT_TILE, LANES), _f32),
            pltpu.VMEM((nb, ATT_TILE, LANES), _f32),
        ],
        compiler_params=pltpu.CompilerParams(
            dimension_semantics=("arbitrary", "arbitrary", "arbitrary"),
            vmem_limit_bytes=VMEM_LIMIT),
        name="attn",
    )(*args)


def _outproj_body(conv_ref, attn_ref, x_ref, wo_ref, g_ref, b_ref, rwt_ref, rb_ref, tri_ref,
                  x1_ref, idx_ref, gate_ref, rank_ref, cnt_ref, carry_ref):
    i = pl.program_id(0)

    @pl.when(i == 0)
    def _():
        carry_ref[...] = jnp.zeros_like(carry_ref)

    mix = jnp.dot(conv_ref[...], wo_ref[0:D_CONV, :], preferred_element_type=_f32)
    mix = mix + jnp.dot(attn_ref[...], wo_ref[D_CONV:, :], preferred_element_type=_f32)
    x1 = _layer_norm(DEEPNORM_ALPHA * x_ref[...] + mix, g_ref[...], b_ref[...])
    x1_ref[...] = x1

    logits = lax.dot_general(rwt_ref[...], x1, (((1,), (1,)), ((), ())),
                             precision=lax.Precision.HIGHEST,
                             preferred_element_type=_f32) + rb_ref[...]
    e_iota = lax.broadcasted_iota(jnp.int32, logits.shape, 0)
    work = logits
    tops, sels = [], []
    for k in range(TOP_K):
        mx = jnp.max(work, axis=0, keepdims=True)
        idx = jnp.min(jnp.where(work == mx, e_iota, N_EXPERTS), axis=0, keepdims=True)
        sel = e_iota == idx
        work = jnp.where(sel, -jnp.inf, work)
        tops.append(mx)
        sels.append(sel)
        idx_ref[k:k + 1, :] = idx
    exps = [jnp.exp(t - tops[0]) for t in tops]
    total = functools.reduce(jnp.add, exps)
    for k in range(TOP_K):
        gate_ref[k:k + 1, :] = exps[k] / total

    chosen = functools.reduce(jnp.logical_or, sels)
    chosen_f = chosen.astype(_f32)
    before = jnp.dot(chosen_f.astype(_bf16), tri_ref[...], preferred_element_type=_f32)
    base = carry_ref[:, 0:1] + before
    for k in range(TOP_K):
        rank = jnp.sum(jnp.where(sels[k], base, 0.0), axis=0, keepdims=True)
        rank_ref[k:k + 1, :] = rank.astype(jnp.int32)
    carry_ref[...] = carry_ref[...] + jnp.sum(chosen_f, axis=1, keepdims=True)
    cnt_ref[...] = carry_ref[...].astype(jnp.int32)


def _outproj(conv2d, attn2d, x2d, w_out_b, ln_g, ln_b, router_wt, router_b):
    t = x2d.shape[0]
    tm = TM_PROJ
    tri = (jnp.arange(tm)[:, None] < jnp.arange(tm)[None, :]).astype(_bf16)
    const = lambda shape: pl.BlockSpec(shape, lambda i: (0, 0))
    tok = lambda rows: pl.BlockSpec((rows, tm), lambda i: (0, i))
    return pl.pallas_call(
        _outproj_body,
        grid=(t // tm,),
        in_specs=[
            pl.BlockSpec((tm, D_CONV), lambda i: (i, 0)),
            pl.BlockSpec((tm, D_ATTN), lambda i: (i, 0)),
            pl.BlockSpec((tm, D_MODEL), lambda i: (i, 0)),
            const((D_MODEL, D_MODEL)), const((1, D_MODEL)), const((1, D_MODEL)),
            const((N_EXPERTS, D_MODEL)), const((N_EXPERTS, 1)), const((tm, tm)),
        ],
        out_specs=[pl.BlockSpec((tm, D_MODEL), lambda i: (i, 0)),
                   tok(TOP_K), tok(TOP_K), tok(TOP_K), const((N_EXPERTS, LANES))],
        out_shape=[jax.ShapeDtypeStruct((t, D_MODEL), _f32),
                   jax.ShapeDtypeStruct((TOP_K, t), jnp.int32),
                   jax.ShapeDtypeStruct((TOP_K, t), _f32),
                   jax.ShapeDtypeStruct((TOP_K, t), jnp.int32),
                   jax.ShapeDtypeStruct((N_EXPERTS, LANES), jnp.int32)],
        scratch_shapes=[pltpu.VMEM((N_EXPERTS, LANES), _f32)],
        compiler_params=pltpu.CompilerParams(dimension_semantics=("arbitrary",),
                                             vmem_limit_bytes=VMEM_LIMIT),
        name="outproj",
    )(conv2d, attn2d, x2d, w_out_b, ln_g.reshape(1, -1), ln_b.reshape(1, -1),
      router_wt, router_b.reshape(-1, 1), tri)


def _dispatch_body(tail_ref, dest_ref, x1_ref, xs_hbm, zeros_ref, sem):
    i = pl.program_id(0)

    @pl.when(i == 0)
    def _():
        zeros_ref[...] = jnp.zeros_like(zeros_ref)

        def tail_copy(e):
            return pltpu.make_async_copy(
                zeros_ref, xs_hbm.at[pl.ds(pl.multiple_of(tail_ref[e], SUBLANES), EXPERT_ROWS)], sem)

        def start(e, c):
            tail_copy(e).start()
            return c

        def wait(e, c):
            tail_copy(e).wait()
            return c

        lax.fori_loop(0, N_EXPERTS, start, 0)
        lax.fori_loop(0, N_EXPERTS, wait, 0)

    def start(it, c):
        for sub in range(ROW_DMA_UNROLL):
            t = it * ROW_DMA_UNROLL + sub
            for k in range(TOP_K):
                pltpu.make_async_copy(x1_ref.at[pl.ds(t, 1)],
                                      xs_hbm.at[pl.ds(dest_ref[k, t], 1)], sem).start()
        return c

    lax.fori_loop(0, TD_DISPATCH // ROW_DMA_UNROLL, start, 0)
    for k in range(TOP_K):
        pltpu.make_async_copy(x1_ref, xs_hbm.at[pl.ds(0, TD_DISPATCH)], sem).wait()


def _dispatch(x1, dest, tail_start, n_rows):
    t = x1.shape[0]
    return pl.pallas_call(
        _dispatch_body,
        grid_spec=pltpu.PrefetchScalarGridSpec(
            num_scalar_prefetch=1,
            grid=(t // TD_DISPATCH,),
            in_specs=[
                pl.BlockSpec((TOP_K, TD_DISPATCH), lambda i, tail: (0, i), memory_space=pltpu.SMEM),
                pl.BlockSpec((TD_DISPATCH, D_MODEL), lambda i, tail: (i, 0)),
            ],
            out_specs=pl.BlockSpec(memory_space=pl.ANY),
            scratch_shapes=[pltpu.VMEM((EXPERT_ROWS, D_MODEL), _f32), pltpu.SemaphoreType.DMA(())],
        ),
        out_shape=jax.ShapeDtypeStruct((n_rows, D_MODEL), _f32),
        compiler_params=pltpu.CompilerParams(dimension_semantics=("arbitrary",),
                                             vmem_limit_bytes=VMEM_LIMIT),
        name="dispatch",
    )(tail_start, dest, x1)


def _expert_body(blk_e_ref, nused_ref, xs_ref, wg_ref, bg_ref, wu_ref, bu_ref, wd_ref, bd_ref,
                 ys_ref, wg_b, wu_b, wd_b):
    n = pl.program_id(0)
    active = n < nused_ref[0]
    prev_e = blk_e_ref[jnp.maximum(n - 1, 0)]
    new_expert = jnp.logical_or(n == 0, blk_e_ref[n] != prev_e)

    @pl.when(jnp.logical_and(active, new_expert))
    def _():
        wg_b[...] = wg_ref[0].astype(_bf16)
        wu_b[...] = wu_ref[0].astype(_bf16)
        wd_b[...] = wd_ref[0].astype(_bf16)

    @pl.when(active)
    def _():
        xb = xs_ref[...].astype(_bf16)
        g = jnp.dot(xb, wg_b[...], preferred_element_type=_f32) + bg_ref[0]
        u = jnp.dot(xb, wu_b[...], preferred_element_type=_f32) + bu_ref[0]
        g = jnp.minimum(g, SWIGLU_LIMIT)
        u = jnp.clip(u, -SWIGLU_LIMIT, SWIGLU_LIMIT)
        hdn = (u + 1.0) * g * jax.nn.sigmoid(SWIGLU_ALPHA * g)
        ys_ref[...] = jnp.dot(hdn.astype(_bf16), wd_b[...], preferred_element_type=_f32) + bd_ref[0]


def _experts(xs, blk_expert, n_used, w_gate, b_gate, w_up, b_up, w_down, b_down):
    n_rows = xs.shape[0]
    n_blocks = n_rows // EXPERT_ROWS

    def row_map(n, blk_e, nused):
        return (jnp.minimum(n, nused[0] - 1), 0)

    def w_map(n, blk_e, nused):
        return (blk_e[jnp.minimum(n, nused[0] - 1)], 0, 0)

    w_spec = pl.BlockSpec((1, D_MODEL, D_FF), w_map)
    b_spec = pl.BlockSpec((1, 1, D_FF), w_map)
    bias = lambda b: b.reshape(N_EXPERTS, 1, -1)
    return pl.pallas_call(
        _expert_body,
        grid_spec=pltpu.PrefetchScalarGridSpec(
            num_scalar_prefetch=2,
            grid=(n_blocks,),
            in_specs=[pl.BlockSpec((EXPERT_ROWS, D_MODEL), row_map),
                      w_spec, b_spec, w_spec, b_spec, w_spec, b_spec],
            out_specs=pl.BlockSpec((EXPERT_ROWS, D_MODEL), row_map),
            scratch_shapes=[pltpu.VMEM((D_MODEL, D_FF), _bf16)] * 3,
        ),
        out_shape=jax.ShapeDtypeStruct((n_rows, D_MODEL), _f32),
        compiler_params=pltpu.CompilerParams(dimension_semantics=("arbitrary",),
                                             vmem_limit_bytes=VMEM_LIMIT),
        name="experts",
    )(blk_expert, n_used, xs, w_gate, bias(b_gate), w_up, bias(b_up), w_down, bias(b_down))


def _combine_body(dest_ref, gate_ref, x1_ref, g_ref, b_ref, ys_hbm, o_ref, buf_ref, sem):
    def start(it, c):
        for sub in range(ROW_DMA_UNROLL):
            t = it * ROW_DMA_UNROLL + sub
            for k in range(TOP_K):
                pltpu.make_async_copy(ys_hbm.at[pl.ds(dest_ref[k, t], 1)],
                                      buf_ref.at[k, pl.ds(t, 1)], sem).start()
        return c

    lax.fori_loop(0, TC_COMBINE // ROW_DMA_UNROLL, start, 0)
    for k in range(TOP_K):
        pltpu.make_async_copy(ys_hbm.at[pl.ds(0, TC_COMBINE)], buf_ref.at[k], sem).wait()

    ffn = gate_ref[:, 0:1] * buf_ref[0]
    for k in range(1, TOP_K):
        ffn = ffn + gate_ref[:, k:k + 1] * buf_ref[k]
    o_ref[...] = _layer_norm(DEEPNORM_ALPHA * x1_ref[...] + ffn, g_ref[...], b_ref[...])


def _combine(dest, gate_tk, x1, ln_g, ln_b, ys):
    t = x1.shape[0]
    tc = TC_COMBINE
    return pl.pallas_call(
        _combine_body,
        grid=(t // tc,),
        in_specs=[
            pl.BlockSpec((TOP_K, tc), lambda i: (0, i), memory_space=pltpu.SMEM),
            pl.BlockSpec((tc, TOP_K), lambda i: (i, 0)),
            pl.BlockSpec((tc, D_MODEL), lambda i: (i, 0)),
            pl.BlockSpec((1, D_MODEL), lambda i: (0, 0)),
            pl.BlockSpec((1, D_MODEL), lambda i: (0, 0)),
            pl.BlockSpec(memory_space=pl.ANY),
        ],
        out_specs=pl.BlockSpec((tc, D_MODEL), lambda i: (i, 0)),
        out_shape=jax.ShapeDtypeStruct((t, D_MODEL), _f32),
        scratch_shapes=[pltpu.VMEM((TOP_K, tc, D_MODEL), _f32), pltpu.SemaphoreType.DMA(())],
        compiler_params=pltpu.CompilerParams(dimension_semantics=("arbitrary",),
                                             vmem_limit_bytes=VMEM_LIMIT),
        name="combine",
    )(dest, gate_tk, x1, ln_g.reshape(1, -1), ln_b.reshape(1, -1), ys)


def kernel(x, positions, w_in, conv_w, conv_b, conv_ln_g, conv_ln_b, conv_pw_w, conv_pw_b, w_out,
           ln1_g, ln1_b, router_w, router_b, exp_w_gate, exp_b_gate, exp_w_up, exp_b_up,
           exp_w_down, exp_b_down, ln2_g, ln2_b):
    batch, seq, _ = x.shape
    t = batch * seq
    for l in range(DEPTH):
        x2d = x.reshape(t, D_MODEL)
        half = jnp.arange(0, HEAD_DIM, 2, dtype=_f32)
        inv_freq = ROPE_THETA ** (-half / HEAD_DIM)
        inv_freq = jnp.tile(inv_freq, 2 * HEADS_PER_GROUP).reshape(1, LANES)

        u, q, k, v = _inproj(x2d, positions.reshape(t, 1), inv_freq, w_in[l].astype(_bf16), batch, seq)
        conv_out = _conv_module(u.reshape(batch, seq, D_CONV), conv_w[l], conv_b[l], conv_ln_g[l],
                                conv_ln_b[l], conv_pw_w[l].astype(_bf16), conv_pw_b[l])
        attn = _attention(q, k, v)
        x1, top_idx, gate, rank, counts = _outproj(
            conv_out.reshape(t, D_CONV), attn.reshape(t, D_ATTN), x2d, w_out[l].astype(_bf16),
            ln1_g[l], ln1_b[l], router_w[l].T, router_b[l])

        counts = counts[:, 0]
        padded = (counts + EXPERT_ROWS - 1) // EXPERT_ROWS * EXPERT_ROWS
        pends = jnp.cumsum(padded)
        pstarts = pends - padded
        n_blocks = (t * TOP_K) // EXPERT_ROWS + N_EXPERTS
        n_rows = n_blocks * EXPERT_ROWS
        experts = jnp.arange(N_EXPERTS, dtype=jnp.int32)[:, None, None]
        dest = jnp.sum(jnp.where(top_idx[None] == experts, pstarts[:, None, None], 0), axis=0) + rank
        tail_start = jnp.minimum(pstarts + counts // SUBLANES * SUBLANES,
                                 n_rows - EXPERT_ROWS).astype(jnp.int32)
        blk_row = jnp.arange(n_blocks, dtype=jnp.int32) * EXPERT_ROWS
        blk_expert = jnp.minimum(jnp.sum(pends[None, :] <= blk_row[:, None], axis=1),
                                 N_EXPERTS - 1).astype(jnp.int32)
        n_used = (pends[-1] // EXPERT_ROWS).astype(jnp.int32).reshape(1)

        xs = _dispatch(x1, dest, tail_start, n_rows)
        ys = _experts(xs, blk_expert, n_used, exp_w_gate[l], exp_b_gate[l], exp_w_up[l], exp_b_up[l],
                      exp_w_down[l], exp_b_down[l])
        out = _combine(dest, gate.T, x1, ln2_g[l], ln2_b[l], ys)
        x = out.reshape(batch, seq, D_MODEL)
    return x
```
